```python
import jax, jax.numpy as jnp
from jax import lax
import numpy as np

D_MODEL = 2048
BATCH = 4
SEQ = 2048
DEPTH = 4
DEC_BATCH = 128
DEC_SEQ = 4
PAST_LEN = 16384
PAGE_SIZE = 128

D_A = D_MODEL
H_A = 8
DG_A = D_A // H_A
CHUNK_A = 128
DK = 128
DV = 128
H_B = D_MODEL // DK
D_BK = H_B * DK
D_BV = H_B * DV
CONV_B = 4
DN_CHUNK = 64
D_C = D_MODEL
CONV_C = 31
D_FF = -(-8 * D_MODEL // (3 * 256)) * 256
EPS = 1e-6

kernel_name = 'hybrid_gmlp_deltanet_conformer_decoder_step'


def _in_sizes():
    return [D_A, D_A, 2 * D_BK + D_BV, D_BV, H_B, H_B, D_C, D_C, D_MODEL, D_MODEL, D_MODEL]


def _in_offsets():
    offs, acc = [], 0
    for s in _in_sizes()[:-1]:
        acc += s
        offs.append(acc)
    return offs


def _rmsnorm(x, g):
    xf = x.astype(jnp.float32)
    y = xf * lax.rsqrt(jnp.mean(xf * xf, axis=-1, keepdims=True) + EPS)
    return (y * g.astype(jnp.float32)).astype(x.dtype)


def _l2norm(x):
    xf = x.astype(jnp.float32)
    return xf * lax.rsqrt(jnp.sum(xf * xf, axis=-1, keepdims=True) + EPS)


def _causal_dwconv(x, buf, w):
    L = x.shape[1]
    xp = jnp.concatenate([buf.astype(x.dtype), x], axis=1)
    y = lax.conv_general_dilated(xp, w[:, None, :].astype(x.dtype), window_strides=(1,), padding='VALID',
                                 dimension_numbers=('NWC', 'WIO', 'NWC'), feature_group_count=x.shape[-1])
    return y, xp[:, L:]


def _chunk_spatial_gate(v, ws, bs):
    B, L, _ = v.shape
    C = min(CHUNK_A, L)
    n = -(-L // C)
    if n * C != L:
        v = jnp.pad(v, ((0, 0), (0, n * C - L), (0, 0)))
    vb = v.reshape(B, n, C, H_A, DG_A)
    w = jnp.where(jnp.tril(jnp.ones((C, C), dtype=bool)), ws[:, :C, :C], 0.0).astype(v.dtype)
    bias = jnp.transpose(bs[:, :C])[None, None, :, :, None].astype(v.dtype)
    out = jnp.einsum('hij,bnjhd->bnihd', w, vb) + bias
    return out.reshape(B, n * C, D_A)[:, :L]


def _gated_delta_rule(q, k, v, beta, g, s0):
    B, L, H, _ = q.shape
    C = min(DN_CHUNK, L)
    n = -(-L // C)
    pad = n * C - L

    def blocks(t):
        t = t.astype(jnp.float32)
        if pad:
            t = jnp.pad(t, ((0, 0), (0, pad)) + ((0, 0),) * (t.ndim - 2))
        t = t.reshape((B, n, C) + t.shape[2:])
        return jnp.moveaxis(t, 3, 1)

    q, k, v, beta, g = blocks(q), blocks(k), blocks(v), blocks(beta), blocks(g)
    gc = jnp.cumsum(g, axis=-1)
    tril = jnp.tril(jnp.ones((C, C), dtype=bool))
    strict = jnp.tril(jnp.ones((C, C), dtype=bool), -1)
    diff = gc[..., :, None] - gc[..., None, :]
    decay = jnp.where(tril, jnp.exp(jnp.where(tril, diff, 0.0)), 0.0)
    kb = k * beta[..., None]
    m = jnp.where(strict, jnp.einsum('bhnid,bhnjd->bhnij', kb, k) * decay, 0.0)
    a = m + jnp.eye(C, dtype=jnp.float32)
    w_k = lax.linalg.triangular_solve(a, kb * jnp.exp(gc)[..., None], left_side=True, lower=True,
                                      unit_diagonal=True)
    u_v = lax.linalg.triangular_solve(a, v * beta[..., None], left_side=True, lower=True, unit_diagonal=True)
    qk = jnp.einsum('bhnid,bhnjd->bhnij', q, k) * decay
    g_last = gc[..., -1]
    k_dec = k * jnp.exp(g_last[..., None] - gc)[..., None]
    q_dec = q * jnp.exp(gc)[..., None]

    def step(S, xs):
        q_c, k_c, u_c, w_c, qk_c, gl = xs
        v_new = u_c - jnp.einsum('bhck,bhkv->bhcv', w_c, S)
        o = jnp.einsum('bhck,bhkv->bhcv', q_c, S) + jnp.einsum('bhij,bhjv->bhiv', qk_c, v_new)
        S = S * jnp.exp(gl)[..., None, None] + jnp.einsum('bhck,bhcv->bhkv', k_c, v_new)
        return S, o

    xs = tuple(jnp.moveaxis(t, 2, 0) for t in (q_dec, k_dec, u_v, w_k, qk, g_last))
    s_fin, o = lax.scan(step, s0.astype(jnp.float32), xs)
    o = jnp.moveaxis(jnp.moveaxis(o, 0, 2), 1, 3).reshape(B, n * C, H, DV)[:, :L]
    return o, s_fin


def _trunk_layer(x, c, s_dn, s_dn_conv, s_cfm_conv, p):
    B, L, _ = x.shape
    mod = (jax.nn.silu(c) @ p['w_ada'] + p['b_ada'])[:, None, :]
    sh1, sc1, gt1, sh2, sc2, gt2 = jnp.split(mod, 6, axis=-1)

    h = _rmsnorm(x, p['norm_mix']) * (1.0 + sc1) + sh1
    proj = h @ p['w_in']
    u, va, qkv, z, beta_raw, alpha_raw, glu_a, glu_b, g_a, g_b, g_c = jnp.split(proj, _in_offsets(), axis=-1)

    va_n = _rmsnorm(jax.nn.gelu(va), p['gmlp_norm'])
    o_a = jax.nn.gelu(u) * _chunk_spatial_gate(va_n, p['gmlp_ws'], p['gmlp_bs'])

    qkv_c, new_dn_conv = _causal_dwconv(qkv, s_dn_conv, p['dn_conv_w'])
    qkv_c = jax.nn.silu(qkv_c)
    q, k, v = jnp.split(qkv_c, [D_BK, 2 * D_BK], axis=-1)
    q = _l2norm(q.reshape(B, L, H_B, DK)) * (DK ** -0.5)
    k = _l2norm(k.reshape(B, L, H_B, DK))
    v = v.reshape(B, L, H_B, DV)
    beta = jax.nn.sigmoid(beta_raw.astype(jnp.float32))
    g = -jnp.exp(p['dn_a_log'].astype(jnp.float32)) * jax.nn.softplus(
        alpha_raw.astype(jnp.float32) + p['dn_dt_bias'].astype(jnp.float32))
    o, new_dn = _gated_delta_rule(q, k, v, beta, g, s_dn)
    o_b = (_rmsnorm(o, p['dn_norm']) * jax.nn.silu(z.reshape(B, L, H_B, DV).astype(jnp.float32)))
    o_b = o_b.reshape(B, L, D_BV).astype(x.dtype)

    glu = glu_a * jax.nn.sigmoid(glu_b)
    conv, new_cfm = _causal_dwconv(glu, s_cfm_conv, p['cfm_conv_w'])
    o_c = jax.nn.silu(_rmsnorm(conv + p['cfm_conv_b'], p['cfm_norm']))

    merged = jax.nn.sigmoid(g_a) * o_a + jax.nn.sigmoid(g_b) * o_b + jax.nn.sigmoid(g_c) * o_c
    x = x + gt1 * (merged @ p['w_out'])

    h2 = _rmsnorm(x, p['norm_ffn']) * (1.0 + sc2) + sh2
    gate, up = jnp.split(h2 @ p['w_gate_up'], 2, axis=-1)
    x = x + gt2 * ((jax.nn.silu(gate) * up) @ p['w_down'])
    return (x, new_dn.astype(s_dn.dtype), new_dn_conv.astype(s_dn_conv.dtype),
            new_cfm.astype(s_cfm_conv.dtype), va_n)


def setup_inputs(seed: int = 0) -> dict:
    key = jax.random.key(seed)
    ks = jax.random.split(key, 32)
    f32 = jnp.float32

    def nrm(k, shape, s):
        return s * jax.random.normal(k, shape, f32)

    def gain(k, shape):
        return 1.0 + 0.02 * jax.random.normal(k, shape, f32)

    dt = jnp.exp(jax.random.uniform(ks[14], (DEPTH, H_B), f32, np.log(1e-3), np.log(1e-1)))
    return {
        'x_prompt': nrm(ks[0], (BATCH, SEQ, D_MODEL), 1.0),
        'x_sample': nrm(ks[1], (DEC_BATCH, DEC_SEQ, D_MODEL), 1.0),
        'state_dn': nrm(ks[2], (DEPTH, DEC_BATCH, H_B, DK, DV), 0.3),
        'state_dn_conv': nrm(ks[3], (DEPTH, DEC_BATCH, CONV_B - 1, 2 * D_BK + D_BV), 1.0),
        'state_cfm_conv': nrm(ks[4], (DEPTH, DEC_BATCH, CONV_C - 1, D_C), 0.5),
        'c_prompt': nrm(ks[5], (BATCH, D_MODEL), 1.0),
        'c_sample': nrm(ks[6], (DEC_BATCH, D_MODEL), 1.0),
        'norm_mix': gain(ks[7], (DEPTH, D_MODEL)),
        'w_ada': nrm(ks[8], (DEPTH, D_MODEL, 6 * D_MODEL), 0.5 * D_MODEL ** -0.5),
        'b_ada': nrm(ks[9], (DEPTH, 6 * D_MODEL), 0.02),
        'w_in': nrm(ks[10], (DEPTH, D_MODEL, sum(_in_sizes())), D_MODEL ** -0.5),
        'gmlp_norm': gain(ks[11], (DEPTH, D_A)),
        'gmlp_ws': nrm(ks[12], (DEPTH, H_A, CHUNK_A, CHUNK_A), CHUNK_A ** -0.5),
        'gmlp_bs': gain(ks[13], (DEPTH, H_A, CHUNK_A)),
        'dn_conv_w': nrm(ks[15], (DEPTH, CONV_B, 2 * D_BK + D_BV), CONV_B ** -0.5),
        'dn_a_log': jnp.log(jax.random.uniform(ks[16], (DEPTH, H_B), f32, 1.0, 16.0)),
        'dn_dt_bias': dt + jnp.log(-jnp.expm1(-dt)),
        'dn_norm': gain(ks[17], (DEPTH, DV)),
        'cfm_conv_w': nrm(ks[18], (DEPTH, CONV_C, D_C), CONV_C ** -0.5),
        'cfm_conv_b': nrm(ks[19], (DEPTH, D_C), 0.02),
        'cfm_norm': gain(ks[20], (DEPTH, D_C)),
        'w_out': nrm(ks[21], (DEPTH, D_MODEL, D_MODEL), D_MODEL ** -0.5),
        'norm_ffn': gain(ks[22], (DEPTH, D_MODEL)),
        'w_gate_up': nrm(ks[23], (DEPTH, D_MODEL, 2 * D_FF), D_MODEL ** -0.5),
        'w_down': nrm(ks[24], (DEPTH, D_FF, D_MODEL), D_FF ** -0.5),
        'norm_final': gain(ks[25], (D_MODEL,)),
    }


def reference(x_prompt, x_sample, state_dn, state_dn_conv, state_cfm_conv, c_prompt, c_sample,
              norm_mix, w_ada, b_ada, w_in, gmlp_norm, gmlp_ws, gmlp_bs, dn_conv_w, dn_a_log, dn_dt_bias,
              dn_norm, cfm_conv_w, cfm_conv_b, cfm_norm, w_out, norm_ffn, w_gate_up, w_down, norm_final):
    bp = x_prompt.shape[0]
    zero_dn = jnp.zeros((bp, H_B, DK, DV), state_dn.dtype)
    zero_dn_conv = jnp.zeros((bp, CONV_B - 1, 2 * D_BK + D_BV), state_dn_conv.dtype)
    zero_cfm = jnp.zeros((bp, CONV_C - 1, D_C), state_cfm_conv.dtype)

    xp, xs = x_prompt, x_sample
    dn_p, dnc_p, cfm_p, dn_s, dnc_s, cfm_s, cv_s = [], [], [], [], [], [], []
    for l in range(DEPTH):
        p = dict(w_ada=w_ada[l], b_ada=b_ada[l], norm_mix=norm_mix[l], w_in=w_in[l], gmlp_norm=gmlp_norm[l],
                 gmlp_ws=gmlp_ws[l], gmlp_bs=gmlp_bs[l], dn_conv_w=dn_conv_w[l], dn_a_log=dn_a_log[l],
                 dn_dt_bias=dn_dt_bias[l], dn_norm=dn_norm[l], cfm_conv_w=cfm_conv_w[l],
                 cfm_conv_b=cfm_conv_b[l], cfm_norm=cfm_norm[l], w_out=w_out[l], norm_ffn=norm_ffn[l],
                 w_gate_up=w_gate_up[l], w_down=w_down[l])
        xp, s1, s2, s3, _ = _trunk_layer(xp, c_prompt, zero_dn, zero_dn_conv, zero_cfm, p)
        dn_p.append(s1); dnc_p.append(s2); cfm_p.append(s3)
        xs, t1, t2, t3, t4 = _trunk_layer(xs, c_sample, state_dn[l], state_dn_conv[l], state_cfm_conv[l], p)
        dn_s.append(t1); dnc_s.append(t2); cfm_s.append(t3); cv_s.append(t4)

    y_prompt = _rmsnorm(xp, norm_final)
    y_sample = _rmsnorm(xs, norm_final)
    return (y_prompt, y_sample, jnp.stack(dn_p), jnp.stack(dnc_p), jnp.stack(cfm_p),
            jnp.stack(dn_s), jnp.stack(dnc_s), jnp.stack(cfm_s), jnp.stack(cv_s))
```

```python
import functools

import jax
import jax.numpy as jnp
from jax import lax
from jax.experimental import pallas as pl
from jax.experimental.pallas import tpu as pltpu

F32 = jnp.float32
BF16 = jnp.bfloat16

D_MODEL = 2048
DK = 128
H_B = D_MODEL // DK
H_A = 8
DG_A = D_MODEL // H_A
CHUNK_A = 128
CONV_B = 4
CONV_C = 31
D_FF = 5632
EPS = 1e-6
N_MAIN = 11 * D_MODEL
BA_LO, BA_HI = 6 * D_MODEL, 6 * D_MODEL + 2 * H_B
G_U, G_VA, G_Q, G_K, G_V, G_Z, G_GLA, G_GLB, G_GA, G_GB, G_GC = range(11)

VMEM_LIMIT = 56 * 1024 * 1024
DN_T = 256


def _cp(n_axes, vmem=VMEM_LIMIT):
    return pltpu.CompilerParams(dimension_semantics=("arbitrary",) * n_axes, vmem_limit_bytes=vmem)


def _rms(x, g):
    return x * lax.rsqrt(jnp.mean(x * x, axis=-1, keepdims=True) + EPS) * g


def _sigmoid(x):
    return jax.nn.sigmoid(x)


def _silu(x):
    return x * jax.nn.sigmoid(x)


def _gelu(x):
    return jax.nn.gelu(x, approximate=True)


def _softplus(x):
    return jnp.maximum(x, 0.0) + jnp.log1p(jnp.exp(-jnp.abs(x)))


def _dot(a, b):
    return jnp.dot(a, b, preferred_element_type=F32)


def _dot_nt(a, b):
    return lax.dot_general(a, b, (((1,), (1,)), ((), ())), preferred_element_type=F32)


def _dot_tn(a, b):
    return lax.dot_general(a, b, (((0,), (0,)), ((), ())), preferred_element_type=F32)


def _ada_kernel(c_ref, w_ref, b_ref, o_ref):
    c = _silu(c_ref[...]).astype(BF16)
    o_ref[...] = _dot(c, w_ref[...].astype(BF16)) + b_ref[...]


def _ada(c_all, w_ada, b_ada, tn=1024):
    depth, k, n = w_ada.shape
    r = c_all.shape[0]
    return pl.pallas_call(
        _ada_kernel,
        grid=(depth, n // tn),
        in_specs=[pl.BlockSpec((r, k), lambda l, j: (0, 0)),
                  pl.BlockSpec((None, k, tn), lambda l, j: (l, 0, j)),
                  pl.BlockSpec((None, 1, tn), lambda l, j: (l, 0, j))],
        out_specs=pl.BlockSpec((None, r, tn), lambda l, j: (l, 0, j)),
        out_shape=jax.ShapeDtypeStruct((depth, r, n), F32),
        compiler_params=_cp(2),
        name="ada",
    )(c_all, w_ada, b_ada.reshape(depth, 1, n))


def _norm_kernel(x_ref, sc_ref, sh_ref, g_ref, h_ref):
    h_ref[...] = (_rms(x_ref[...], g_ref[...]) * (1.0 + sc_ref[...]) + sh_ref[...]).astype(BF16)


def _mm_kernel(x_ref, w_ref, o_ref):
    o_ref[...] = _dot(x_ref[...], w_ref[...]).astype(o_ref.dtype)


def _matmul(x, w_stack, l, tm, tn, name):
    m, k = x.shape
    n = w_stack.shape[2]
    tm = min(tm, m)
    tn = min(tn, n)
    return pl.pallas_call(
        _mm_kernel,
        grid=(m // tm, n // tn),
        in_specs=[pl.BlockSpec((tm, k), lambda i, j: (i, 0)),
                  pl.BlockSpec((None, k, tn), lambda i, j: (l, 0, j))],
        out_specs=pl.BlockSpec((tm, tn), lambda i, j: (i, j)),
        out_shape=jax.ShapeDtypeStruct((m, n), F32),
        compiler_params=_cp(2),
        name=name,
    )(x, w_stack)


def _gu_kernel(x_ref, wg_ref, wu_ref, o_ref):
    x = x_ref[...]
    g = _dot(x, wg_ref[...])
    u = _dot(x, wu_ref[...])
    o_ref[...] = (_silu(g) * u).astype(BF16)


def _gate_up(h, w_gu, l, tm, tn=512):
    m, k = h.shape
    tm = min(tm, m)
    nj = D_FF // tn
    return pl.pallas_call(
        _gu_kernel,
        grid=(m // tm, nj),
        in_specs=[pl.BlockSpec((tm, k), lambda i, j: (i, 0)),
                  pl.BlockSpec((None, k, tn), lambda i, j: (l, 0, j)),
                  pl.BlockSpec((None, k, tn), lambda i, j: (l, 0, j + nj))],
        out_specs=pl.BlockSpec((tm, tn), lambda i, j: (i, j)),
        out_shape=jax.ShapeDtypeStruct((m, D_FF), BF16),
        compiler_params=_cp(2),
        name="gate_up",
    )(h, w_gu, w_gu)


def _mod_spec(kind, l, chunk, tm, tiles_per_b, n_extra_axes=0):
    if kind == "prompt":
        if n_extra_axes:
            return pl.BlockSpec((None, None, 1, D_MODEL), lambda i, k: (l, i // tiles_per_b, 0, chunk))
        return pl.BlockSpec((None, None, 1, D_MODEL), lambda i: (l, i // tiles_per_b, 0, chunk))
    if n_extra_axes:
        return pl.BlockSpec((None, tm, D_MODEL), lambda i, k: (l, 0, chunk))
    return pl.BlockSpec((None, tm, D_MODEL), lambda i: (l, 0, chunk))


def _vec_spec(l, n_extra_axes=0):
    if n_extra_axes:
        return pl.BlockSpec((None, 1, D_MODEL), lambda i, k: (l, 0, 0))
    return pl.BlockSpec((None, 1, D_MODEL), lambda i: (l, 0, 0))


def _norm0(x, mod, norm_mix, kind, tm, tiles_per_b):
    m = x.shape[0]
    return pl.pallas_call(
        _norm_kernel,
        grid=(m // tm,),
        in_specs=[pl.BlockSpec((tm, D_MODEL), lambda i: (i, 0)),
                  _mod_spec(kind, 0, 1, tm, tiles_per_b),
                  _mod_spec(kind, 0, 0, tm, tiles_per_b),
                  _vec_spec(0)],
        out_specs=pl.BlockSpec((tm, D_MODEL), lambda i: (i, 0)),
        out_shape=jax.ShapeDtypeStruct((m, D_MODEL), BF16),
        compiler_params=_cp(1),
        name="norm0",
    )(x, mod, mod, norm_mix)


def _out_kernel(m_ref, w_ref, x_ref, gt_ref, sc_ref, sh_ref, g_ref, x1_ref, h_ref):
    x1 = x_ref[...] + gt_ref[...] * _dot(m_ref[...], w_ref[...])
    x1_ref[...] = x1
    h_ref[...] = (_rms(x1, g_ref[...]) * (1.0 + sc_ref[...]) + sh_ref[...]).astype(BF16)


def _out_proj(merged, w_out, x, mod, norm_ffn, l, kind, tm, tiles_per_b):
    m = x.shape[0]
    return pl.pallas_call(
        _out_kernel,
        grid=(m // tm,),
        in_specs=[pl.BlockSpec((tm, D_MODEL), lambda i: (i, 0)),
                  pl.BlockSpec((None, D_MODEL, D_MODEL), lambda i: (l, 0, 0)),
                  pl.BlockSpec((tm, D_MODEL), lambda i: (i, 0)),
                  _mod_spec(kind, l, 2, tm, tiles_per_b),
                  _mod_spec(kind, l, 4, tm, tiles_per_b),
                  _mod_spec(kind, l, 3, tm, tiles_per_b),
                  _vec_spec(l)],
        out_specs=[pl.BlockSpec((tm, D_MODEL), lambda i: (i, 0)),
                   pl.BlockSpec((tm, D_MODEL), lambda i: (i, 0))],
        out_shape=[jax.ShapeDtypeStruct((m, D_MODEL), F32),
                   jax.ShapeDtypeStruct((m, D_MODEL), BF16)],
        compiler_params=_cp(1),
        name="out_proj",
    )(merged, w_out, x, mod, mod, mod, norm_ffn)


def _down_mid_kernel(a_ref, w_ref, x_ref, gt_ref, sc_ref, sh_ref, g_ref, x2_ref, h_ref, acc_ref):
    k = pl.program_id(1)

    @pl.when(k == 0)
    def _():
        acc_ref[...] = jnp.zeros_like(acc_ref)

    acc_ref[...] += _dot(a_ref[...], w_ref[...])

    @pl.when(k == pl.num_programs(1) - 1)
    def _():
        x2 = x_ref[...] + gt_ref[...] * acc_ref[...]
        x2_ref[...] = x2
        h_ref[...] = (_rms(x2, g_ref[...]) * (1.0 + sc_ref[...]) + sh_ref[...]).astype(BF16)


def _down_last_kernel(a_ref, w_ref, x_ref, gt_ref, g_ref, y_ref, acc_ref):
    k = pl.program_id(1)

    @pl.when(k == 0)
    def _():
        acc_ref[...] = jnp.zeros_like(acc_ref)

    acc_ref[...] += _dot(a_ref[...], w_ref[...])

    @pl.when(k == pl.num_programs(1) - 1)
    def _():
        x2 = x_ref[...] + gt_ref[...] * acc_ref[...]
        y_ref[...] = _rms(x2, g_ref[...])


def _down_proj(act, w_down, x, mod, norm_next, l, last, kind, tm, tiles_per_b, tk=512):
    m = x.shape[0]
    nk = D_FF // tk
    row = pl.BlockSpec((tm, D_MODEL), lambda i, k: (i, 0))
    a_spec = pl.BlockSpec((tm, tk), lambda i, k: (i, k))
    w_spec = pl.BlockSpec((None, tk, D_MODEL), lambda i, k: (l, k, 0))
    if last:
        return pl.pallas_call(
            _down_last_kernel,
            grid=(m // tm, nk),
            in_specs=[a_spec, w_spec, row, _mod_spec(kind, l, 5, tm, tiles_per_b, 1),
                      pl.BlockSpec((1, D_MODEL), lambda i, k: (0, 0))],
            out_specs=row,
            out_shape=jax.ShapeDtypeStruct((m, D_MODEL), F32),
            scratch_shapes=[pltpu.VMEM((tm, D_MODEL), F32)],
            compiler_params=_cp(2),
            name="down_last",
        )(act, w_down, x, mod, norm_next)
    return pl.pallas_call(
        _down_mid_kernel,
        grid=(m // tm, nk),
        in_specs=[a_spec, w_spec, row,
                  _mod_spec(kind, l, 5, tm, tiles_per_b, 1),
                  _mod_spec(kind, l + 1, 1, tm, tiles_per_b, 1),
                  _mod_spec(kind, l + 1, 0, tm, tiles_per_b, 1),
                  _vec_spec(l + 1, 1)],
        out_specs=[row, row],
        out_shape=[jax.ShapeDtypeStruct((m, D_MODEL), F32),
                   jax.ShapeDtypeStruct((m, D_MODEL), BF16)],
        scratch_shapes=[pltpu.VMEM((tm, D_MODEL), F32)],
        compiler_params=_cp(2),
        name="down_mid",
    )(act, w_down, x, mod, mod, mod, norm_next)


def _dn_prompt_kernel(q_ref, k_ref, v_ref, z_ref, ba_ref, cwq_ref, cwk_ref, cwv_ref, alog_ref, dtb_ref,
                      nrm_ref, o_ref, s_ref, cq_ref, ck_ref, cv_ref, S_scr, buf_scr, gT_scr):
    hp = pl.program_id(1)
    t = pl.program_id(2)
    last_t = pl.num_programs(2) - 1
    T = DN_T

    @pl.when(t == 0)
    def _():
        S_scr[...] = jnp.zeros_like(S_scr)
        buf_scr[:, 0:8, :] = jnp.zeros((3, 8, 2 * DK), F32)

    def conv(i, x_ref, w_ref):
        buf_scr[i, 8:8 + T, :] = x_ref[...]
        w = w_ref[...]
        y = w[3:4] * buf_scr[i, 8:8 + T, :]
        for kk in range(CONV_B - 1):
            y = y + w[kk:kk + 1] * buf_scr[i, 5 + kk:5 + kk + T, :]
        return _silu(y)

    qc = conv(0, q_ref, cwq_ref)
    kc = conv(1, k_ref, cwk_ref)
    vc = conv(2, v_ref, cwv_ref)

    @pl.when(t == last_t)
    def _():
        cq_ref[...] = buf_scr[0, 8 + T - 3:8 + T, :]
        ck_ref[...] = buf_scr[1, 8 + T - 3:8 + T, :]
        cv_ref[...] = buf_scr[2, 8 + T - 3:8 + T, :]

    buf_scr[:, 0:8, :] = buf_scr[:, T:T + 8, :]

    ba = ba_ref[...]
    beta_all = _sigmoid(ba)
    g_all = -jnp.exp(alog_ref[...]) * _softplus(ba + dtb_ref[...])
    row_i = lax.broadcasted_iota(jnp.int32, (T, T), 0)
    col_i = lax.broadcasted_iota(jnp.int32, (T, T), 1)
    tril = row_i >= col_i
    strict = row_i > col_i
    ltri = jnp.where(tril, 1.0, 0.0).astype(F32)
    gc_all = jnp.dot(ltri, g_all, preferred_element_type=F32, precision=lax.Precision.HIGHEST)
    gT_scr[...] = gc_all.T
    lane = lax.broadcasted_iota(jnp.int32, (T, DK), 1)
    z = z_ref[...]

    for j in range(2):
        hh = hp * 2 + j
        sl = slice(j * DK, (j + 1) * DK)
        beta = jnp.sum(jnp.where(lane == hh, beta_all, 0.0), axis=1, keepdims=True)
        gc = jnp.sum(jnp.where(lane == hh + H_B, gc_all, 0.0), axis=1, keepdims=True)
        gc_row = gT_scr[pl.ds(hh + H_B, 1), :]
        q = qc[:, sl]
        k = kc[:, sl]
        v = vc[:, sl]
        q = q * lax.rsqrt(jnp.sum(q * q, axis=1, keepdims=True) + EPS) * (DK ** -0.5)
        k = k * lax.rsqrt(jnp.sum(k * k, axis=1, keepdims=True) + EPS)
        decay = jnp.where(tril, jnp.exp(jnp.where(tril, gc - gc_row, 0.0)), 0.0)
        kb = k * beta
        kbf = k.astype(BF16)
        m_mat = jnp.where(strict, _dot_nt(kb.astype(BF16), kbf) * decay, 0.0)
        qk = jnp.where(tril, _dot_nt(q.astype(BF16), kbf) * decay, 0.0)
        egc = jnp.exp(gc)
        t_inv = jnp.where(row_i == col_i, 1.0, 0.0) - jnp.where((row_i >> 1) == (col_i >> 1), m_mat, 0.0)
        for s in range(1, T.bit_length() - 1):
            f_mat = jnp.where(((row_i >> (s + 1)) == (col_i >> (s + 1))) & ((row_i >> s) != (col_i >> s)),
                              m_mat, 0.0)
            tb = t_inv.astype(BF16)
            t_inv = t_inv - _dot(tb, _dot(f_mat.astype(BF16), tb).astype(BF16))
        x = _dot(t_inv.astype(BF16), jnp.concatenate([kb * egc, v * beta], axis=1).astype(BF16))
        w = x[:, :DK]
        u = x[:, DK:]
        s_old = S_scr[j]
        sb = s_old.astype(BF16)
        v_new = u - _dot(w.astype(BF16), sb)
        o = _dot((q * egc).astype(BF16), sb) + _dot(qk.astype(BF16), v_new.astype(BF16))
        g_last = gc[T - 1:T, :]
        k_dec = k * jnp.exp(g_last - gc)
        s_new = s_old * jnp.exp(g_last) + _dot_tn(k_dec.astype(BF16), v_new.astype(BF16))
        S_scr[j] = s_new
        o_ref[:, sl] = _rms(o, nrm_ref[...]) * _silu(z[:, sl])

        @pl.when(t == last_t)
        def _():
            s_ref[j] = s_new


def _dn_prompt(proj, ba, dn_conv_w, alog_p, dtb_p, dn_norm, l, bsz, seq):
    nt = seq // DN_T
    hp_n = H_B // 2
    w2 = 2 * DK
    cpb = D_MODEL // w2

    def col(group):
        return pl.BlockSpec((DN_T, w2), lambda b, h, t: (b * nt + t, group * cpb + h))

    def cw(part):
        return pl.BlockSpec((None, CONV_B, w2), lambda b, h, t: (l, 0, part * cpb + h))

    vec = pl.BlockSpec((None, 1, DK), lambda b, h, t: (l, 0, 0))
    cs = pl.BlockSpec((None, CONV_B - 1, w2), lambda b, h, t: (b, 0, h))
    cs_shape = jax.ShapeDtypeStruct((bsz, CONV_B - 1, D_MODEL), F32)
    return pl.pallas_call(
        _dn_prompt_kernel,
        grid=(bsz, hp_n, nt),
        in_specs=[col(G_Q), col(G_K), col(G_V), col(G_Z),
                  pl.BlockSpec((DN_T, DK), lambda b, h, t: (b * nt + t, 0)),
                  cw(0), cw(1), cw(2), vec, vec, vec],
        out_specs=[pl.BlockSpec((DN_T, w2), lambda b, h, t: (b * nt + t, h)),
                   pl.BlockSpec((None, 2, DK, DK), lambda b, h, t: (b, h, 0, 0)),
                   cs, cs, cs],
        out_shape=[jax.ShapeDtypeStruct((bsz * seq, D_MODEL), F32),
                   jax.ShapeDtypeStruct((bsz, H_B, DK, DK), F32),
                   cs_shape, cs_shape, cs_shape],
        scratch_shapes=[pltpu.VMEM((2, DK, DK), F32),
                        pltpu.VMEM((3, DN_T + 8, w2), F32),
                        pltpu.VMEM((DK, DN_T), F32)],
        compiler_params=_cp(3),
        name="dn_prompt",
    )(proj, proj, proj, proj, ba, dn_conv_w, dn_conv_w, dn_conv_w, alog_p, dtb_p, dn_norm)


ACM_T = CHUNK_A
ACM_HIST = 32
ACM_CW = 256


def _acm_prompt_kernel(u_ref, va_ref, gla_ref, glb_ref, ga_ref, gb_ref, gc_ref, ob_ref, gn_ref, ws_ref,
                       bs_ref, cw_ref, cb_ref, cn_ref, m_ref, st_ref, buf_scr, y_scr):
    t = pl.program_id(1)
    T = ACM_T

    @pl.when(t == 0)
    def _():
        buf_scr[0:ACM_HIST, :] = jnp.zeros((ACM_HIST, D_MODEL), F32)

    buf_scr[ACM_HIST:ACM_HIST + T, :] = gla_ref[...] * _sigmoid(glb_ref[...])
    base = ACM_HIST - (CONV_C - 1)

    def strip(c, carry):
        cs = pl.ds(pl.multiple_of(c * ACM_CW, ACM_CW), ACM_CW)
        acc = cw_ref[0:1, cs] * buf_scr[base:base + T, cs]
        for kk in range(1, CONV_C):
            acc = acc + cw_ref[kk:kk + 1, cs] * buf_scr[base + kk:base + kk + T, cs]
        y_scr[:, cs] = acc
        return carry

    lax.fori_loop(0, D_MODEL // ACM_CW, strip, 0)

    @pl.when(t == pl.num_programs(1) - 1)
    def _():
        st_ref[...] = buf_scr[ACM_HIST + T - (CONV_C - 1):ACM_HIST + T, :]

    buf_scr[0:ACM_HIST, :] = buf_scr[T:T + ACM_HIST, :]
    o_c = _silu(_rms(y_scr[...] + cb_ref[...], cn_ref[...]))

    va_n = _rms(_gelu(va_ref[...]), gn_ref[...]).astype(BF16)
    row_i = lax.broadcasted_iota(jnp.int32, (T, T), 0)
    col_i = lax.broadcasted_iota(jnp.int32, (T, T), 1)
    tril = row_i >= col_i
    bs = bs_ref[...]
    for h in range(H_A):
        sl = slice(h * DG_A, (h + 1) * DG_A)
        wm = jnp.where(tril, ws_ref[h], 0.0).astype(BF16)
        gate = _dot(wm, va_n[:, sl]) + bs[:, h:h + 1]
        o_a = _gelu(u_ref[:, sl]) * gate
        m_ref[:, sl] = (_sigmoid(ga_ref[:, sl]) * o_a + _sigmoid(gb_ref[:, sl]) * ob_ref[:, sl]
                        + _sigmoid(gc_ref[:, sl]) * o_c[:, sl]).astype(BF16)


def _acm_prompt(proj, o_b, gmlp_norm, gmlp_ws, gmlp_bs_t, cfm_w, cfm_b, cfm_norm, l, bsz, seq):
    nt = seq // ACM_T

    def col(group):
        return pl.BlockSpec((ACM_T, D_MODEL), lambda b, t: (b * nt + t, group))

    vec = pl.BlockSpec((None, 1, D_MODEL), lambda b, t: (l, 0, 0))
    return pl.pallas_call(
        _acm_prompt_kernel,
        grid=(bsz, nt),
        in_specs=[col(G_U), col(G_VA), col(G_GLA), col(G_GLB), col(G_GA), col(G_GB), col(G_GC),
                  pl.BlockSpec((ACM_T, D_MODEL), lambda b, t: (b * nt + t, 0)),
                  vec,
                  pl.BlockSpec((None, H_A, CHUNK_A, CHUNK_A), lambda b, t: (l, 0, 0, 0)),
                  pl.BlockSpec((None, CHUNK_A, H_A), lambda b, t: (l, 0, 0)),
                  pl.BlockSpec((None, CONV_C, D_MODEL), lambda b, t: (l, 0, 0)),
                  vec, vec],
        out_specs=[pl.BlockSpec((ACM_T, D_MODEL), lambda b, t: (b * nt + t, 0)),
                   pl.BlockSpec((None, CONV_C - 1, D_MODEL), lambda b, t: (b, 0, 0))],
        out_shape=[jax.ShapeDtypeStruct((bsz * seq, D_MODEL), BF16),
                   jax.ShapeDtypeStruct((bsz, CONV_C - 1, D_MODEL), F32)],
        scratch_shapes=[pltpu.VMEM((ACM_HIST + ACM_T, D_MODEL), F32),
                        pltpu.VMEM((ACM_T, D_MODEL), F32)],
        compiler_params=_cp(2),
        name="acm_prompt",
    )(proj, proj, proj, proj, proj, proj, proj, o_b, gmlp_norm, gmlp_ws, gmlp_bs_t, cfm_w, cfm_b, cfm_norm)


def _dn_sample_kernel(q_ref, k_ref, v_ref, z_ref, ba_ref, sq_ref, sk_ref, sv_ref, cwq_ref, cwk_ref, cwv_ref,
                      alog_ref, dtb_ref, nrm_ref, s_ref,
                      o_ref, so_ref, cq_ref, ck_ref, cv_ref,
                      lhs_scr, r_scr, a_scr, v_scr, e_scr):
    L = q_ref.shape[0]
    bb = q_ref.shape[1]
    hh = pl.program_id(1)
    lane = lax.broadcasted_iota(jnp.int32, (bb, DK), 1)

    def col(x, idx):
        return jnp.sum(jnp.where(lane == idx, x, 0.0), axis=1, keepdims=True)

    def conv(x_ref, st_ref, w_ref, cs_ref):
        xp = [st_ref[i] for i in range(CONV_B - 1)] + [x_ref[i] for i in range(L)]
        w = w_ref[...]
        out = []
        for i in range(L):
            y = w[0:1] * xp[i]
            for kk in range(1, CONV_B):
                y = y + w[kk:kk + 1] * xp[i + kk]
            out.append(_silu(y))
        for i in range(CONV_B - 1):
            cs_ref[i] = xp[L + i]
        return out

    qc = conv(q_ref, sq_ref, cwq_ref, cq_ref)
    kc = conv(k_ref, sk_ref, cwk_ref, ck_ref)
    vc = conv(v_ref, sv_ref, cwv_ref, cv_ref)

    neg_a = -jnp.exp(alog_ref[...])
    q, k, kb, vb, gc = [], [], [], [], []
    run = None
    for i in range(L):
        ba = ba_ref[i]
        beta = col(_sigmoid(ba), hh)
        g = col(neg_a * _softplus(ba + dtb_ref[...]), hh + H_B)
        run = g if run is None else run + g
        gc.append(run)
        qi = qc[i]
        ki = kc[i]
        q.append(qi * lax.rsqrt(jnp.sum(qi * qi, axis=1, keepdims=True) + EPS) * (DK ** -0.5))
        ki = ki * lax.rsqrt(jnp.sum(ki * ki, axis=1, keepdims=True) + EPS)
        k.append(ki)
        kb.append(ki * beta)
        vb.append(vc[i] * beta)

    w, u = [], []
    for i in range(L):
        wi = kb[i] * jnp.exp(gc[i])
        ui = vb[i]
        for j in range(i):
            m_ij = jnp.sum(kb[i] * k[j], axis=1, keepdims=True) * jnp.exp(gc[i] - gc[j])
            wi = wi - m_ij * w[j]
            ui = ui - m_ij * u[j]
        w.append(wi)
        u.append(ui)

    zero = jnp.zeros((bb, DK), F32)
    for i in range(8):
        if i < L:
            lhs_scr[pl.ds(i, bb, stride=8), :] = w[i]
        elif i < 2 * L:
            lhs_scr[pl.ds(i, bb, stride=8), :] = q[i - L] * jnp.exp(gc[i - L])
        else:
            lhs_scr[pl.ds(i, bb, stride=8), :] = zero

    def prod_state(b, carry):
        rows = pl.ds(pl.multiple_of(b * 8, 8), 8)
        r_scr[rows, :] = _dot(lhs_scr[rows, :].astype(BF16), s_ref[b].astype(BF16))
        return carry

    lax.fori_loop(0, bb, prod_state, 0)

    v_new = [u[i] - r_scr[pl.ds(i, bb, stride=8), :] for i in range(L)]
    nrm = nrm_ref[...]
    for i in range(L):
        o = r_scr[pl.ds(L + i, bb, stride=8), :]
        for j in range(i + 1):
            qk_ij = jnp.sum(q[i] * k[j], axis=1, keepdims=True) * jnp.exp(gc[i] - gc[j])
            o = o + qk_ij * v_new[j]
        o_ref[i] = _rms(o, nrm) * _silu(z_ref[i])

    g_last = gc[L - 1]
    for i in range(8):
        if i < L:
            a_scr[pl.ds(i, bb, stride=8), :] = k[i] * jnp.exp(g_last - gc[i])
            v_scr[pl.ds(i, bb, stride=8), :] = v_new[i]
        else:
            a_scr[pl.ds(i, bb, stride=8), :] = zero
            v_scr[pl.ds(i, bb, stride=8), :] = zero
    e_scr[...] = jnp.broadcast_to(jnp.exp(g_last), (bb, DK))

    def update_state(b, carry):
        rows = pl.ds(pl.multiple_of(b * 8, 8), 8)
        upd = _dot_tn(a_scr[rows, :].astype(BF16), v_scr[rows, :].astype(BF16))
        so_ref[b] = s_ref[b] * e_scr[pl.ds(b, 1), :] + upd
        return carry

    lax.fori_loop(0, bb, update_state, 0)


def _dn_sample(proj3, ba3, state_dn, sdc_t, dn_conv_w, alog_p, dtb_p, dn_norm, l, bb):
    L, bsz, _ = proj3.shape
    cpg = D_MODEL // DK

    def col(group):
        return pl.BlockSpec((L, bb, DK), lambda i, h: (0, i, group * cpg + h))

    def st(part):
        return pl.BlockSpec((None, CONV_B - 1, bb, DK), lambda i, h: (l, 0, i, part * cpg + h))

    def cw(part):
        return pl.BlockSpec((None, CONV_B, DK), lambda i, h: (l, 0, part * cpg + h))

    vec = pl.BlockSpec((None, 1, DK), lambda i, h: (l, 0, 0))
    cs = pl.BlockSpec((CONV_B - 1, bb, DK), lambda i, h: (0, i, h))
    cs_shape = jax.ShapeDtypeStruct((CONV_B - 1, bsz, D_MODEL), F32)
    return pl.pallas_call(
        _dn_sample_kernel,
        grid=(bsz // bb, H_B),
        in_specs=[col(G_Q), col(G_K), col(G_V), col(G_Z),
                  pl.BlockSpec((L, bb, DK), lambda i, h: (0, i, 0)),
                  st(0), st(1), st(2), cw(0), cw(1), cw(2), vec, vec, vec,
                  pl.BlockSpec((None, bb, None, DK, DK), lambda i, h: (l, i, h, 0, 0))],
        out_specs=[pl.BlockSpec((L, bb, DK), lambda i, h: (0, i, h)),
                   pl.BlockSpec((bb, None, DK, DK), lambda i, h: (i, h, 0, 0)),
                   cs, cs, cs],
        out_shape=[jax.ShapeDtypeStruct((L, bsz, D_MODEL), F32),
                   jax.ShapeDtypeStruct((bsz, H_B, DK, DK), F32),
                   cs_shape, cs_shape, cs_shape],
        scratch_shapes=[pltpu.VMEM((bb * 8, DK), F32)] * 4 + [pltpu.VMEM((bb, DK), F32)],
        compiler_params=_cp(2),
        name="dn_sample",
    )(proj3, proj3, proj3, proj3, ba3, sdc_t, sdc_t, sdc_t, dn_conv_w, dn_conv_w, dn_conv_w,
      alog_p, dtb_p, dn_norm, state_dn)


def _acm_sample_kernel(u_ref, va_ref, gla_ref, glb_ref, ga_ref, gb_ref, gc_ref, ob_ref, st_ref, gn_ref,
                       ws_ref, bs_ref, cw_ref, cb_ref, cn_ref, m_ref, cv_ref, so_ref, glu_scr):
    L = u_ref.shape[0]
    hist = CONV_C - 1
    gn = gn_ref[...]
    for i in range(L):
        glu_scr[i] = gla_ref[i] * _sigmoid(glb_ref[i])
        cv_ref[i] = _rms(_gelu(va_ref[i]), gn)

    def xp(i):
        return st_ref[i] if i < hist else glu_scr[i - hist]

    for i in range(hist):
        so_ref[i] = xp(i + L)

    cb = cb_ref[...]
    cn = cn_ref[...]
    for i in range(L):
        acc = cw_ref[0:1, :] * xp(i)
        for kk in range(1, CONV_C):
            acc = acc + cw_ref[kk:kk + 1, :] * xp(i + kk)
        o_c = _silu(_rms(acc + cb, cn))
        gate = bs_ref[i:i + 1, :]
        for j in range(i + 1):
            gate = gate + ws_ref[i * L + j:i * L + j + 1, :] * cv_ref[j]
        o_a = _gelu(u_ref[i]) * gate
        m_ref[i] = (_sigmoid(ga_ref[i]) * o_a + _sigmoid(gb_ref[i]) * ob_ref[i]
                    + _sigmoid(gc_ref[i]) * o_c).astype(BF16)


def _acm_sample(proj3, o_b3, cfm_t, gmlp_norm, ws_s, bs_s, cfm_w, cfm_b, cfm_norm, l, bb):
    L, bsz, _ = proj3.shape

    def col(group):
        return pl.BlockSpec((L, bb, D_MODEL), lambda i: (0, i, group))

    vec = pl.BlockSpec((None, 1, D_MODEL), lambda i: (l, 0, 0))
    slab = pl.BlockSpec((L, bb, D_MODEL), lambda i: (0, i, 0))
    return pl.pallas_call(
        _acm_sample_kernel,
        grid=(bsz // bb,),
        in_specs=[col(G_U), col(G_VA), col(G_GLA), col(G_GLB), col(G_GA), col(G_GB), col(G_GC), slab,
                  pl.BlockSpec((None, CONV_C - 1, bb, D_MODEL), lambda i: (l, 0, i, 0)),
                  vec,
                  pl.BlockSpec((None, L * L, D_MODEL), lambda i: (l, 0, 0)),
                  pl.BlockSpec((None, L, D_MODEL), lambda i: (l, 0, 0)),
                  pl.BlockSpec((None, CONV_C, D_MODEL), lambda i: (l, 0, 0)),
                  vec, vec],
        out_specs=[slab, slab, pl.BlockSpec((CONV_C - 1, bb, D_MODEL), lambda i: (0, i, 0))],
        out_shape=[jax.ShapeDtypeStruct((L, bsz, D_MODEL), BF16),
                   jax.ShapeDtypeStruct((L, bsz, D_MODEL), F32),
                   jax.ShapeDtypeStruct((CONV_C - 1, bsz, D_MODEL), F32)],
        scratch_shapes=[pltpu.VMEM((L, bb, D_MODEL), F32)],
        compiler_params=_cp(1),
        name="acm_sample",
    )(proj3, proj3, proj3, proj3, proj3, proj3, proj3, o_b3, cfm_t, gmlp_norm, ws_s, bs_s, cfm_w, cfm_b,
      cfm_norm)


def _row3(a):
    return a.reshape(a.shape[0], 1, a.shape[1])


def kernel(x_prompt, x_sample, state_dn, state_dn_conv, state_cfm_conv, c_prompt, c_sample, norm_mix, w_ada,
           b_ada, w_in, gmlp_norm, gmlp_ws, gmlp_bs, dn_conv_w, dn_a_log, dn_dt_bias, dn_norm, cfm_conv_w,
           cfm_conv_b, cfm_norm, w_out, norm_ffn, w_gate_up, w_down, norm_final):
    depth = w_in.shape[0]
    bp, lp, _ = x_prompt.shape
    bs, ls, _ = x_sample.shape
    mp = bp * lp
    ms = bs * ls

    w_main = jnp.concatenate([w_in[:, :, :BA_LO], w_in[:, :, BA_HI:]], axis=2).astype(BF16)
    w_ba = jnp.pad(w_in[:, :, BA_LO:BA_HI], ((0, 0), (0, 0), (0, DK - 2 * H_B))).astype(BF16)
    w_out_b = w_out.astype(BF16)
    w_gu_b = w_gate_up.astype(BF16)
    w_down_b = w_down.astype(BF16)
    pad_hb = ((0, 0), (H_B, DK - 2 * H_B))
    alog_p = _row3(jnp.pad(dn_a_log, pad_hb))
    dtb_p = _row3(jnp.pad(dn_dt_bias, pad_hb))
    dn_norm3 = _row3(dn_norm)
    norm_mix3, norm_ffn3 = _row3(norm_mix), _row3(norm_ffn)
    gmlp_norm3, cfm_b3, cfm_norm3 = _row3(gmlp_norm), _row3(cfm_conv_b), _row3(cfm_norm)
    gmlp_bs_t = jnp.swapaxes(gmlp_bs, 1, 2)
    ws_s = jnp.repeat(jnp.transpose(gmlp_ws[:, :, :ls, :ls], (0, 2, 3, 1)).reshape(depth, ls * ls, H_A),
                      DG_A, axis=2)
    bs_s = jnp.repeat(jnp.transpose(gmlp_bs[:, :, :ls], (0, 2, 1)), DG_A, axis=2)
    sdc_t = jnp.swapaxes(state_dn_conv, 1, 2)
    cfm_t = jnp.swapaxes(state_cfm_conv, 1, 2)

    s_off = -(-bp // 8) * 8
    rows = s_off + bs
    c_all = jnp.zeros((rows, D_MODEL), F32).at[:bp].set(c_prompt).at[s_off:].set(c_sample)
    mod = _ada(c_all, w_ada, b_ada)
    mod_p = mod[:, :bp].reshape(depth, bp, 1, 6 * D_MODEL)
    mod_s = mod[:, s_off:]

    tm_p = min(512, lp)
    tm_o = min(256, lp)
    tpb = lp // tm_p
    xp = x_prompt.reshape(mp, D_MODEL)
    xs = jnp.swapaxes(x_sample, 0, 1).reshape(ms, D_MODEL)
    hp = _norm0(xp, mod_p, norm_mix3, "prompt", tm_p, tpb)
    hs = _norm0(xs, mod_s, norm_mix3, "sample", bs, 1)

    bb_dn = min(64, bs)
    bb_acm = min(16, bs)
    outs = {k: [] for k in ("dn_p", "dnc_p", "cfm_p", "dn_s", "dnc_s", "cfm_s", "cv_s")}
    for l in range(depth):
        last = l == depth - 1
        proj = _matmul(hp, w_main, l, 1024, 1024, "in_proj")
        ba = _matmul(hp, w_ba, l, 1024, DK, "in_proj_ba")
        o_b, s_new, cq, ck, cv = _dn_prompt(proj, ba, dn_conv_w, alog_p, dtb_p, dn_norm3, l, bp, lp)
        merged, cfm_new = _acm_prompt(proj, o_b, gmlp_norm3, gmlp_ws, gmlp_bs_t, cfm_conv_w, cfm_b3,
                                      cfm_norm3, l, bp, lp)
        x1, h2 = _out_proj(merged, w_out_b, xp, mod_p, norm_ffn3, l, "prompt", tm_o, lp // tm_o)
        act = _gate_up(h2, w_gu_b, l, 1024)
        if last:
            y_prompt = _down_proj(act, w_down_b, x1, mod_p, norm_final.reshape(1, D_MODEL), l, True,
                                  "prompt", tm_p, tpb)
        else:
            xp, hp = _down_proj(act, w_down_b, x1, mod_p, norm_mix3, l, False, "prompt", tm_p, tpb)
        outs["dn_p"].append(s_new)
        outs["dnc_p"].append(jnp.concatenate([cq, ck, cv], axis=2))
        outs["cfm_p"].append(cfm_new)

        proj_s = _matmul(hs, w_main, l, 512, 1024, "in_proj")
        ba_s = _matmul(hs, w_ba, l, 512, DK, "in_proj_ba")
        proj3 = proj_s.reshape(ls, bs, N_MAIN)
        ob3, s_new, cq, ck, cv = _dn_sample(proj3, ba_s.reshape(ls, bs, DK), state_dn, sdc_t, dn_conv_w,
                                            alog_p, dtb_p, dn_norm3, l, bb_dn)
        merged3, cv3, cfm_new = _acm_sample(proj3, ob3, cfm_t, gmlp_norm3, ws_s, bs_s, cfm_conv_w, cfm_b3,
                                            cfm_norm3, l, bb_acm)
        x1, h2 = _out_proj(merged3.reshape(ms, D_MODEL), w_out_b, xs, mod_s, norm_ffn3, l, "sample", bs, 1)
        act = _gate_up(h2, w_gu_b, l, 512)
        if last:
            y_sample = _down_proj(act, w_down_b, x1, mod_s, norm_final.reshape(1, D_MODEL), l, True,
                                  "sample", bs, 1)
        else:
            xs, hs = _down_proj(act, w_down_b, x1, mod_s, norm_mix3, l, False, "sample", bs, 1)
        outs["dn_s"].append(s_new)
        outs["dnc_s"].append(jnp.swapaxes(jnp.concatenate([cq, ck, cv], axis=2), 0, 1))
        outs["cfm_s"].append(jnp.swapaxes(cfm_new, 0, 1))
        outs["cv_s"].append(jnp.swapaxes(cv3, 0, 1))

    y_prompt = y_prompt.reshape(bp, lp, D_MODEL)
    y_sample = jnp.swapaxes(y_sample.reshape(ls, bs, D_MODEL), 0, 1)
    return (y_prompt, y_sample, jnp.stack(outs["dn_p"]), jnp.stack(outs["dnc_p"]), jnp.stack(outs["cfm_p"]),
            jnp.stack(outs["dn_s"]), jnp.stack(outs["dnc_s"]), jnp.stack(outs["cfm_s"]),
            jnp.stack(outs["cv_s"]))
```
